```python
import jax, jax.numpy as jnp
from jax import lax
import numpy as np

D_MODEL = 2048
BATCH = 4
SEQ = 8192
DEPTH = 1
DEC_BATCH = 32
DEC_SEQ = 64
PAST_LEN = 2048

CHUNK = 64
E_POOL = 1024
N_POOL_GROUPS = 4
POOL_GROUP = E_POOL // N_POOL_GROUPS
POOL_WINDOWS = (2, 4, 8, 16)
POOL_BUF = max(POOL_WINDOWS) - 1
E_CONV = 1024
CONV_WIDTH = 3
CONV_BUF = CONV_WIDTH - 1
SPLIT_SIZES = (E_POOL, E_POOL, E_CONV, E_CONV, E_CONV, E_CONV, D_MODEL, D_MODEL)
IN_COLS = sum(SPLIT_SIZES)
SPLIT_IDX = tuple(int(i) for i in np.cumsum(SPLIT_SIZES)[:-1])
EPS = 1e-6

kernel_name = 'hybrid_pool_shortconv_gated_stream_step'


def rmsnorm(x, g):
    xf = x.astype(jnp.float32)
    r = lax.rsqrt(jnp.mean(xf * xf, axis=-1, keepdims=True) + EPS)
    return (xf * r * g.astype(jnp.float32)).astype(x.dtype)


def pool_mixer(xa, buf, offset, w_mix, scale):
    B, T, _ = xa.shape
    xp = jnp.concatenate([buf.astype(xa.dtype), xa], axis=1).astype(jnp.float32)
    cs = jnp.pad(jnp.cumsum(xp, axis=1), ((0, 0), (1, 0), (0, 0)))
    pos = offset + jnp.arange(T, dtype=jnp.int32)
    outs = []
    for g, w in enumerate(POOL_WINDOWS):
        sl = slice(g * POOL_GROUP, (g + 1) * POOL_GROUP)
        s = cs[:, POOL_BUF + 1:POOL_BUF + 1 + T, sl] - cs[:, POOL_BUF + 1 - w:POOL_BUF + 1 - w + T, sl]
        cnt = jnp.minimum(pos + 1, w).astype(jnp.float32)[None, :, None]
        outs.append(s / cnt - xp[:, POOL_BUF:, sl])
    d = jnp.stack(outs, axis=2)
    mixed = jnp.einsum('btgc,gcd->btgd', d, w_mix.astype(jnp.float32))
    y = mixed.reshape(B, T, E_POOL) * scale.astype(jnp.float32)
    return y.astype(xa.dtype), xp[:, -POOL_BUF:].astype(xa.dtype)


def conv_mixer(u, buf, w_conv):
    T = u.shape[1]
    up = jnp.concatenate([buf.astype(u.dtype), u], axis=1)
    y = w_conv[0] * up[:, 0:T] + w_conv[1] * up[:, 1:1 + T] + w_conv[2] * up[:, 2:2 + T]
    return y, up[:, -CONV_BUF:]


def layer(x, pool_buf, conv_buf, offset, norm_g, w_in, b_gate, w_pool_mix, pool_scale,
          w_conv, w_proj_pool, w_proj_conv, w_out):
    h = rmsnorm(x, norm_g)
    proj = jnp.einsum('btd,dc->btc', h, w_in)
    xa, za, v, bg, cg, zb, ga, gb = jnp.split(proj, SPLIT_IDX, axis=-1)
    ya, new_pool = pool_mixer(xa, pool_buf, offset, w_pool_mix, pool_scale)
    ya = ya * jax.nn.silu(za)
    yc, new_conv = conv_mixer(cg * v, conv_buf, w_conv)
    yb = bg * yc * jax.nn.silu(zb)
    merged = (jax.nn.sigmoid(ga + b_gate[:D_MODEL]) * jnp.einsum('bte,ed->btd', ya, w_proj_pool)
              + jax.nn.sigmoid(gb + b_gate[D_MODEL:]) * jnp.einsum('bte,ed->btd', yb, w_proj_conv))
    x = x + jnp.einsum('btd,de->bte', merged, w_out)
    return x, new_pool, new_conv


def trunk(x, pool_bufs, conv_bufs, offset, norm_g, w_in, b_gate, w_pool_mix, pool_scale,
          w_conv, w_proj_pool, w_proj_conv, w_out, final_norm_g):
    new_pools, new_convs = [], []
    for l in range(DEPTH):
        x, npool, nconv = layer(x, pool_bufs[l], conv_bufs[l], offset, norm_g[l], w_in[l], b_gate[l],
                                w_pool_mix[l], pool_scale[l], w_conv[l], w_proj_pool[l],
                                w_proj_conv[l], w_out[l])
        new_pools.append(npool)
        new_convs.append(nconv)
    return rmsnorm(x, final_norm_g), jnp.stack(new_pools, axis=0), jnp.stack(new_convs, axis=0)


def setup_inputs(seed: int = 0) -> dict:
    key = jax.random.key(seed)
    ks = jax.random.split(key, 14)
    f32 = jnp.float32
    return {
        'x_prompt': jax.random.normal(ks[0], (BATCH, SEQ, D_MODEL), f32),
        'x_sample': jax.random.normal(ks[1], (DEC_BATCH, DEC_SEQ, D_MODEL), f32),
        'state_pool': jax.random.normal(ks[2], (DEPTH, DEC_BATCH, POOL_BUF, E_POOL), f32),
        'state_conv': jax.random.normal(ks[3], (DEPTH, DEC_BATCH, CONV_BUF, E_CONV), f32),
        'norm_g': 1.0 + 0.02 * jax.random.normal(ks[4], (DEPTH, D_MODEL), f32),
        'w_in': jax.random.normal(ks[5], (DEPTH, D_MODEL, IN_COLS), f32) * D_MODEL ** -0.5,
        'b_gate': 0.02 * jax.random.normal(ks[6], (DEPTH, 2 * D_MODEL), f32),
        'w_pool_mix': jax.random.normal(ks[7], (DEPTH, N_POOL_GROUPS, POOL_GROUP, POOL_GROUP), f32) * POOL_GROUP ** -0.5,
        'pool_scale': 1.0 + 0.02 * jax.random.normal(ks[8], (DEPTH, E_POOL), f32),
        'w_conv': jax.random.normal(ks[9], (DEPTH, CONV_WIDTH, E_CONV), f32) * CONV_WIDTH ** -0.5,
        'w_proj_pool': jax.random.normal(ks[10], (DEPTH, E_POOL, D_MODEL), f32) * E_POOL ** -0.5,
        'w_proj_conv': jax.random.normal(ks[11], (DEPTH, E_CONV, D_MODEL), f32) * E_CONV ** -0.5,
        'w_out': jax.random.normal(ks[12], (DEPTH, D_MODEL, D_MODEL), f32) * D_MODEL ** -0.5,
        'final_norm_g': 1.0 + 0.02 * jax.random.normal(ks[13], (D_MODEL,), f32),
    }


def reference(x_prompt, x_sample, state_pool, state_conv, norm_g, w_in, b_gate, w_pool_mix,
              pool_scale, w_conv, w_proj_pool, w_proj_conv, w_out, final_norm_g):
    bp = x_prompt.shape[0]
    zero_pool = jnp.zeros((DEPTH, bp, POOL_BUF, E_POOL), x_prompt.dtype)
    zero_conv = jnp.zeros((DEPTH, bp, CONV_BUF, E_CONV), x_prompt.dtype)
    y_prompt, new_pool_prompt, new_conv_prompt = trunk(
        x_prompt, zero_pool, zero_conv, 0, norm_g, w_in, b_gate, w_pool_mix, pool_scale,
        w_conv, w_proj_pool, w_proj_conv, w_out, final_norm_g)
    y_sample, new_pool_sample, new_conv_sample = trunk(
        x_sample, state_pool, state_conv, PAST_LEN, norm_g, w_in, b_gate, w_pool_mix, pool_scale,
        w_conv, w_proj_pool, w_proj_conv, w_out, final_norm_g)
    return (y_prompt, y_sample, new_pool_prompt, new_conv_prompt, new_pool_sample, new_conv_sample)
```

```python
import functools

import jax
import jax.numpy as jnp
from jax import lax
from jax.experimental import pallas as pl
from jax.experimental.pallas import tpu as pltpu

D_MODEL = 2048
E_POOL = 1024
E_CONV = 1024
N_POOL_GROUPS = 4
POOL_GROUP = E_POOL // N_POOL_GROUPS
POOL_WINDOWS = (2, 4, 8, 16)
POOL_BUF = max(POOL_WINDOWS) - 1
CONV_WIDTH = 3
CONV_BUF = CONV_WIDTH - 1
PAST_LEN = 2048
EPS = 1e-6

POOL_HIST = 16
CONV_HIST = 8
BRANCH_COLS = 2 * E_POOL + 4 * E_CONV
COL_XA, COL_ZA = 0, E_POOL
COL_V, COL_BG, COL_CG, COL_ZB = (2 * E_POOL + i * E_CONV for i in range(4))

ROW_TILE = 512
COL_CHUNK = 512
VMEM_LIMIT = 56 * 1024 * 1024

_BF16 = jnp.bfloat16
_F32 = jnp.float32


def _dot(a, b):
    return jnp.dot(a, b, preferred_element_type=_F32)


def _rmsnorm(x, g):
    r = lax.rsqrt(jnp.mean(x * x, axis=-1, keepdims=True) + EPS)
    return x * r * g


def _silu(z):
    return z * jax.nn.sigmoid(z)


def _resident(shape):
    zeros = (0,) * len(shape)
    return pl.BlockSpec(shape, lambda *_: zeros, pipeline_mode=pl.Buffered(1))


def _branch_kernel(x_ref, ph_ref, ch_ref, g_ref, w_ref, wmix_ref, ps_ref, wc_ref,
                   ya_ref, yb_ref, np_ref, nc_ref,
                   h_scr, xa_ext, u_ext, d_scr, yc_scr, *, sb, ts, nt, offset):
    t = pl.program_id(1)
    h_scr[...] = _rmsnorm(x_ref[...], g_ref[...]).astype(_BF16)

    def load_history():
        for s in range(sb):
            xa_ext[s, 0:POOL_HIST, :] = ph_ref[s]
            u_ext[s, 0:CONV_HIST, :] = ch_ref[s]

    def carry_history():
        for s in range(sb):
            xa_ext[s, 0:POOL_HIST, :] = xa_ext[s, ts:ts + POOL_HIST, :]
            u_ext[s, 0:CONV_HIST, :] = u_ext[s, ts:ts + CONV_HIST, :]

    if nt == 1:
        load_history()
    else:
        pl.when(t == 0)(load_history)
        pl.when(t > 0)(carry_history)

    h = h_scr[...]
    pos = offset + t * ts + lax.broadcasted_iota(jnp.int32, (ts, 1), 0)

    for c0 in range(0, E_POOL, COL_CHUNK):
        cols = slice(c0, c0 + COL_CHUNK)
        xa = _dot(h, w_ref[:, COL_XA + c0:COL_XA + c0 + COL_CHUNK])
        for s in range(sb):
            xa_ext[s, POOL_HIST:POOL_HIST + ts, cols] = xa[s * ts:(s + 1) * ts]
        za = _dot(h, w_ref[:, COL_ZA + c0:COL_ZA + c0 + COL_CHUNK])
        for g in range(c0 // POOL_GROUP, (c0 + COL_CHUNK) // POOL_GROUP):
            w = POOL_WINDOWS[g]
            lanes = slice(g * POOL_GROUP, (g + 1) * POOL_GROUP)
            inv_cnt = 1.0 / jnp.minimum(pos + 1, w).astype(_F32)
            for s in range(sb):
                cur = xa_ext[s, POOL_HIST:POOL_HIST + ts, lanes]
                acc = cur
                for j in range(1, w):
                    acc = acc + xa_ext[s, POOL_HIST - j:POOL_HIST - j + ts, lanes]
                d_scr[s * ts:(s + 1) * ts, lanes] = (acc * inv_cnt - cur).astype(_BF16)
            mixed = _dot(d_scr[:, lanes], wmix_ref[g])
            zl = slice(g * POOL_GROUP - c0, (g + 1) * POOL_GROUP - c0)
            ya = mixed * ps_ref[:, lanes] * _silu(za[:, zl])
            ya_ref[:, lanes] = ya.astype(_BF16)

    for c0 in range(0, E_CONV, COL_CHUNK):
        cols = slice(c0, c0 + COL_CHUNK)
        v = _dot(h, w_ref[:, COL_V + c0:COL_V + c0 + COL_CHUNK])
        cg = _dot(h, w_ref[:, COL_CG + c0:COL_CG + c0 + COL_CHUNK])
        u = cg * v
        for s in range(sb):
            u_ext[s, CONV_HIST:CONV_HIST + ts, cols] = u[s * ts:(s + 1) * ts]
        for s in range(sb):
            yc = (wc_ref[0:1, cols] * u_ext[s, CONV_HIST - 2:CONV_HIST - 2 + ts, cols]
                  + wc_ref[1:2, cols] * u_ext[s, CONV_HIST - 1:CONV_HIST - 1 + ts, cols]
                  + wc_ref[2:3, cols] * u_ext[s, CONV_HIST:CONV_HIST + ts, cols])
            yc_scr[s * ts:(s + 1) * ts, :] = yc
        bg = _dot(h, w_ref[:, COL_BG + c0:COL_BG + c0 + COL_CHUNK])
        zb = _dot(h, w_ref[:, COL_ZB + c0:COL_ZB + c0 + COL_CHUNK])
        yb_ref[:, cols] = (bg * yc_scr[...] * _silu(zb)).astype(_BF16)

    for s in range(sb):
        np_ref[s] = xa_ext[s, ts:ts + POOL_HIST, :]
        nc_ref[s] = u_ext[s, ts:ts + CONV_HIST, :]


def _merge_kernel(x_ref, ya_ref, yb_ref, g_ref, wg_ref, b_ref, wpp_ref, wpc_ref, m_ref, h_scr):
    h_scr[...] = _rmsnorm(x_ref[...], g_ref[...]).astype(_BF16)
    h = h_scr[...]
    ya = ya_ref[...]
    yb = yb_ref[...]
    for c0 in range(0, D_MODEL, COL_CHUNK):
        ca = slice(c0, c0 + COL_CHUNK)
        cb = slice(D_MODEL + c0, D_MODEL + c0 + COL_CHUNK)
        ga = _dot(h, wg_ref[:, ca]) + b_ref[:, ca]
        a = _dot(ya, wpp_ref[:, ca])
        gb = _dot(h, wg_ref[:, cb]) + b_ref[:, cb]
        b = _dot(yb, wpc_ref[:, ca])
        m_ref[:, ca] = (jax.nn.sigmoid(ga) * a + jax.nn.sigmoid(gb) * b).astype(_BF16)


def _out_kernel(x_ref, m_ref, wo_ref, fg_ref, y_ref):
    m = m_ref[...]
    for c0 in range(0, D_MODEL, COL_CHUNK):
        cols = slice(c0, c0 + COL_CHUNK)
        y_ref[:, cols] = x_ref[:, cols] + _dot(m, wo_ref[:, cols])
    y_ref[...] = _rmsnorm(y_ref[...], fg_ref[...])


def _trunk(x, pool_hist, conv_hist, offset, sb, ts, p):
    bsz, seq, _ = x.shape
    nt = seq // ts
    nb = bsz // sb
    m = sb * ts
    n = bsz * seq
    x2 = x.reshape(n, D_MODEL)

    row_spec = lambda width: pl.BlockSpec((m, width), lambda b, t: (b * nt + t, 0))
    hist_spec = lambda rows, width: pl.BlockSpec((sb, rows, width), lambda b, t: (b, 0, 0))
    ya, yb, new_pool, new_conv = pl.pallas_call(
        functools.partial(_branch_kernel, sb=sb, ts=ts, nt=nt, offset=offset),
        grid=(nb, nt),
        in_specs=[
            row_spec(D_MODEL),
            hist_spec(POOL_HIST, E_POOL),
            hist_spec(CONV_HIST, E_CONV),
            _resident((1, D_MODEL)),
            _resident((D_MODEL, BRANCH_COLS)),
            _resident((N_POOL_GROUPS, POOL_GROUP, POOL_GROUP)),
            _resident((1, E_POOL)),
            _resident((CONV_WIDTH, E_CONV)),
        ],
        out_specs=[
            row_spec(E_POOL),
            row_spec(E_CONV),
            hist_spec(POOL_HIST, E_POOL),
            hist_spec(CONV_HIST, E_CONV),
        ],
        out_shape=[
            jax.ShapeDtypeStruct((n, E_POOL), _BF16),
            jax.ShapeDtypeStruct((n, E_CONV), _BF16),
            jax.ShapeDtypeStruct((bsz, POOL_HIST, E_POOL), _F32),
            jax.ShapeDtypeStruct((bsz, CONV_HIST, E_CONV), _F32),
        ],
        scratch_shapes=[
            pltpu.VMEM((m, D_MODEL), _BF16),
            pltpu.VMEM((sb, POOL_HIST + ts, E_POOL), _F32),
            pltpu.VMEM((sb, CONV_HIST + ts, E_CONV), _F32),
            pltpu.VMEM((m, E_POOL), _BF16),
            pltpu.VMEM((m, COL_CHUNK), _F32),
        ],
        compiler_params=pltpu.CompilerParams(
            dimension_semantics=("arbitrary", "arbitrary"), vmem_limit_bytes=VMEM_LIMIT),
        name="branches",
    )(x2, pool_hist, conv_hist, p["norm_g"], p["w_branch"], p["w_mix"], p["pool_scale"], p["w_conv"])

    rows = lambda width: pl.BlockSpec((ROW_TILE, width), lambda i: (i, 0))
    merged = pl.pallas_call(
        _merge_kernel,
        grid=(n // ROW_TILE,),
        in_specs=[
            rows(D_MODEL), rows(E_POOL), rows(E_CONV),
            _resident((1, D_MODEL)),
            _resident((D_MODEL, 2 * D_MODEL)),
            _resident((1, 2 * D_MODEL)),
            _resident((E_POOL, D_MODEL)),
            _resident((E_CONV, D_MODEL)),
        ],
        out_specs=rows(D_MODEL),
        out_shape=jax.ShapeDtypeStruct((n, D_MODEL), _BF16),
        scratch_shapes=[pltpu.VMEM((ROW_TILE, D_MODEL), _BF16)],
        compiler_params=pltpu.CompilerParams(
            dimension_semantics=("arbitrary",), vmem_limit_bytes=VMEM_LIMIT),
        name="merge",
    )(x2, ya, yb, p["norm_g"], p["w_gate"], p["b_gate"], p["w_proj_pool"], p["w_proj_conv"])

    y = pl.pallas_call(
        _out_kernel,
        grid=(n // ROW_TILE,),
        in_specs=[rows(D_MODEL), rows(D_MODEL), _resident((D_MODEL, D_MODEL)), _resident((1, D_MODEL))],
        out_specs=rows(D_MODEL),
        out_shape=jax.ShapeDtypeStruct((n, D_MODEL), _F32),
        compiler_params=pltpu.CompilerParams(
            dimension_semantics=("arbitrary",), vmem_limit_bytes=VMEM_LIMIT),
        name="output",
    )(x2, merged, p["w_out"], p["final_norm_g"])

    return (y.reshape(bsz, seq, D_MODEL),
            new_pool[None, :, POOL_HIST - POOL_BUF:, :],
            new_conv[None, :, CONV_HIST - CONV_BUF:, :])


def _pad_history(state, rows):
    return jnp.pad(state, ((0, 0), (rows - state.shape[1], 0), (0, 0)))


def kernel(x_prompt, x_sample, state_pool, state_conv, norm_g, w_in, b_gate, w_pool_mix,
           pool_scale, w_conv, w_proj_pool, w_proj_conv, w_out, final_norm_g):
    assert norm_g.shape[0] == 1, "single-layer trunk"
    p = {
        "norm_g": norm_g[0][None, :],
        "w_branch": w_in[0][:, :BRANCH_COLS].astype(_BF16),
        "w_gate": w_in[0][:, BRANCH_COLS:].astype(_BF16),
        "b_gate": b_gate[0][None, :],
        "w_mix": w_pool_mix[0].astype(_BF16),
        "pool_scale": pool_scale[0][None, :],
        "w_conv": w_conv[0],
        "w_proj_pool": w_proj_pool[0].astype(_BF16),
        "w_proj_conv": w_proj_conv[0].astype(_BF16),
        "w_out": w_out[0].astype(_BF16),
        "final_norm_g": final_norm_g[None, :],
    }
    bp = x_prompt.shape[0]
    zero_pool = jnp.zeros((bp, POOL_HIST, E_POOL), _F32)
    zero_conv = jnp.zeros((bp, CONV_HIST, E_CONV), _F32)
    y_prompt, np_prompt, nc_prompt = _trunk(x_prompt, zero_pool, zero_conv, 0, 1, ROW_TILE, p)

    ds = x_sample.shape[1]
    y_sample, np_sample, nc_sample = _trunk(
        x_sample, _pad_history(state_pool[0], POOL_HIST), _pad_history(state_conv[0], CONV_HIST),
        PAST_LEN, ROW_TILE // ds, ds, p)
    return (y_prompt, y_sample, np_prompt, nc_prompt, np_sample, nc_sample)
```
